```python
import math
import jax, jax.numpy as jnp
from jax import lax
import numpy as np

D_MODEL = 1024
BATCH = 2
SEQ = 8192
DEPTH = 4
DEC_BATCH = 128
DEC_SEQ = 1
PAST_LEN = 2048
PAGE_SIZE = 128

N_MIXERS = 2
N_A_LAYERS = (DEPTH + N_MIXERS - 1) // N_MIXERS
N_B_LAYERS = DEPTH // N_MIXERS
HEAD_DIM = 64
H_A = D_MODEL // HEAD_DIM
KV_A = H_A // 4
G_A = H_A // KV_A
H_I = H_A // 2
D_IDX = 64
K_TOP_MAX = 256
H_B = D_MODEL // HEAD_DIM
N_META = 16
N_BUCKETS = 32
MAX_DISTANCE = 128
D_FF = 2816
CONV_W = 3
BLOCK = 128
EPS = 1e-6
A_SPLITS = (H_A * HEAD_DIM, KV_A * HEAD_DIM, KV_A * HEAD_DIM, H_I * D_IDX, D_IDX, H_I)
A_IN = sum(A_SPLITS)
B_SPLITS = (H_B * HEAD_DIM, H_B * HEAD_DIM, H_B * HEAD_DIM)
B_IN = sum(B_SPLITS)

kernel_name = 'dsa_stickbreak_hybrid_step'


def _split(p, sizes):
    return jnp.split(p, [int(c) for c in np.cumsum(sizes)[:-1]], axis=-1)


def _rms_norm(x, g):
    xf = x.astype(jnp.float32)
    y = xf * lax.rsqrt(jnp.mean(xf * xf, axis=-1, keepdims=True) + EPS)
    return (y * g.astype(jnp.float32)).astype(x.dtype)


def _take_rows(rows, idx):
    return jax.vmap(lambda r, i: r[i])(rows, idx)


def _t5_bucket(dist):
    max_exact = N_BUCKETS // 2
    d = jnp.maximum(dist, 1).astype(jnp.float32)
    large = max_exact + (jnp.log(d / max_exact) / math.log(MAX_DISTANCE / max_exact)
                         * (N_BUCKETS - max_exact)).astype(jnp.int32)
    return jnp.where(dist < max_exact, dist, jnp.minimum(large, N_BUCKETS - 1))


def _dsa_project(h, w_in):
    Bn, T, _ = h.shape
    q, k, v, qi, ki, wi = _split(h @ w_in, A_SPLITS)
    return (q.reshape(Bn, T, H_A, HEAD_DIM), k.reshape(Bn, T, KV_A, HEAD_DIM),
            v.reshape(Bn, T, KV_A, HEAD_DIM), qi.reshape(Bn, T, H_I, D_IDX), ki, wi)


def _dsa_core(q, qi, wi, qpos, kidx, kpos, kvalid, gather_kv, bias_table, k_top):
    Bn, Q = q.shape[0], q.shape[1]
    s = jnp.einsum('bqhd,bsd->bqhs', qi, kidx, preferred_element_type=jnp.float32) * (D_IDX ** -0.5)
    score = jnp.einsum('bqhs,bqh->bqs', jax.nn.relu(s), wi.astype(jnp.float32)) * (H_I ** -0.5)
    visible = (kpos[None, :] <= jnp.maximum(qpos, 0)[:, None]) & kvalid[None, :]
    score = jnp.where(visible[None], score, -jnp.inf)
    top_val, top_idx = lax.top_k(score, k_top)
    sel_ok = jnp.isfinite(top_val)
    k_sel, v_sel = gather_kv(top_idx)
    dist = jnp.maximum(qpos[None, :, None] - kpos[top_idx], 0)
    bias = bias_table[_t5_bucket(dist)]
    bias = jnp.transpose(bias.reshape(Bn, Q, k_top, KV_A, G_A), (0, 1, 3, 4, 2))
    qg = q.reshape(Bn, Q, KV_A, G_A, HEAD_DIM)
    logits = jnp.einsum('bqcgd,bqncd->bqcgn', qg, k_sel, preferred_element_type=jnp.float32) * (HEAD_DIM ** -0.5)
    logits = jnp.where(sel_ok[:, :, None, None, :], logits + bias.astype(jnp.float32), -jnp.inf)
    p = jax.nn.softmax(logits, axis=-1)
    o = jnp.einsum('bqcgn,bqncd->bqcgd', p.astype(v_sel.dtype), v_sel)
    return o.reshape(Bn, Q, H_A * HEAD_DIM)


def _dsa_prompt(h, w_in, w_out, bias_table, pos, valid, k_top):
    Bn, T, _ = h.shape
    q, k, v, qi, ki, wi = _dsa_project(h, w_in)

    def gather_kv(idx):
        return _take_rows(k, idx), _take_rows(v, idx)

    def block(i):
        st = i * BLOCK
        sl = lambda a: lax.dynamic_slice_in_dim(a, st, BLOCK, axis=1)
        return _dsa_core(sl(q), sl(qi), sl(wi), lax.dynamic_slice_in_dim(pos, st, BLOCK, axis=0),
                         ki, pos, valid, gather_kv, bias_table, k_top)

    o = lax.map(block, jnp.arange(T // BLOCK))
    o = jnp.moveaxis(o, 0, 1).reshape(Bn, T, H_A * HEAD_DIM)
    return o @ w_out, k, v, ki


def _dsa_sample(h, w_in, w_out, bias_table, cache_k, cache_v, cache_kidx, li, page_table, k_top):
    DB, S, _ = h.shape
    q, k, v, qi, ki, wi = _dsa_project(h, w_in)
    past_len = page_table.shape[1] * PAGE_SIZE
    L = past_len + S
    kidx_all = jnp.concatenate([cache_kidx[li, page_table].reshape(DB, past_len, D_IDX), ki], axis=1)
    kpos = jnp.arange(L)
    qpos = past_len + jnp.arange(S)

    def gather_kv(idx):
        is_past = (idx < past_len)[..., None, None]
        pidx = jnp.minimum(idx, past_len - 1)
        phys = _take_rows(page_table, pidx // PAGE_SIZE)
        off = pidx % PAGE_SIZE
        nidx = jnp.clip(idx - past_len, 0, S - 1)
        k_sel = jnp.where(is_past, cache_k[li, phys, off], _take_rows(k, nidx))
        v_sel = jnp.where(is_past, cache_v[li, phys, off], _take_rows(v, nidx))
        return k_sel, v_sel

    o = _dsa_core(q, qi, wi, qpos, kidx_all, kpos, jnp.ones((L,), dtype=bool), gather_kv, bias_table, k_top)
    return o @ w_out, k, v, ki


def _sb_project(h, w_in):
    Bn, T, _ = h.shape
    q, k, v = _split(h @ w_in, B_SPLITS)
    shp = (Bn, T, H_B, HEAD_DIM)
    return q.reshape(shp), k.reshape(shp), v.reshape(shp)


def _sb_core(q, k, v, qpos, kpos, kvalid):
    z = jnp.einsum('bqhd,bshd->bhqs', q, k, preferred_element_type=jnp.float32) * (HEAD_DIM ** -0.5)
    mask = ((kpos[None, :] < qpos[:, None]) & kvalid[None, :])[None, None]
    log_1m = jnp.where(mask, jax.nn.log_sigmoid(-z), 0.0)
    shifted = jnp.concatenate([log_1m[..., 1:], jnp.zeros_like(log_1m[..., :1])], axis=-1)
    suffix = lax.cumsum(shifted, axis=3, reverse=True)
    a = jnp.where(mask, jnp.exp(jax.nn.log_sigmoid(z) + suffix), 0.0)
    return jnp.einsum('bhqs,bshd->bqhd', a.astype(v.dtype), v)


def _sb_prompt(h, w_in, w_out, pos, valid):
    Bn, T, _ = h.shape
    q, k, v = _sb_project(h, w_in)

    def block(i):
        st = i * BLOCK
        return _sb_core(lax.dynamic_slice_in_dim(q, st, BLOCK, axis=1), k, v,
                        lax.dynamic_slice_in_dim(pos, st, BLOCK, axis=0), pos, valid)

    o = lax.map(block, jnp.arange(T // BLOCK))
    o = jnp.moveaxis(o, 0, 1).reshape(Bn, T, H_B * HEAD_DIM)
    return o @ w_out, k, v


def _sb_sample(h, w_in, w_out, cache_k, cache_v, li, page_table):
    DB, S, _ = h.shape
    q, k, v = _sb_project(h, w_in)
    past_len = page_table.shape[1] * PAGE_SIZE
    L = past_len + S
    k_all = jnp.concatenate([cache_k[li, page_table].reshape(DB, past_len, H_B, HEAD_DIM), k], axis=1)
    v_all = jnp.concatenate([cache_v[li, page_table].reshape(DB, past_len, H_B, HEAD_DIM), v], axis=1)
    o = _sb_core(q, k_all, v_all, past_len + jnp.arange(S), jnp.arange(L), jnp.ones((L,), dtype=bool))
    return o.reshape(DB, S, H_B * HEAD_DIM) @ w_out, k, v


def _conv_ffn(h, prev, row_mask, w_in, conv_w, conv_b, w_out):
    T = h.shape[1]
    u = (h @ w_in) * row_mask[None, :, None].astype(h.dtype)
    ext = jnp.concatenate([prev.astype(u.dtype), u], axis=1)
    c = conv_b + sum(conv_w[j] * ext[:, j:j + T] for j in range(CONV_W))
    g, val = jnp.split(c, 2, axis=-1)
    return (jax.nn.silu(g) * val) @ w_out, ext[:, -(CONV_W - 1):]


def setup_inputs(seed: int = 0) -> dict:
    key = jax.random.key(seed)
    ks = jax.random.split(key, 24)
    n_pages = PAST_LEN // PAGE_SIZE
    n_pool = (DEC_BATCH * n_pages * 5) // 4
    nrm = lambda k, shp, s=1.0: jax.random.normal(k, shp, dtype=jnp.float32) * s
    page_table = jax.random.permutation(ks[7], n_pool)[:DEC_BATCH * n_pages].reshape(DEC_BATCH, n_pages).astype(jnp.int32)
    return {
        'x_prompt': nrm(ks[0], (BATCH, SEQ, D_MODEL)),
        'x_sample': nrm(ks[1], (DEC_BATCH, DEC_SEQ, D_MODEL)),
        'cache_dsa_k': nrm(ks[2], (N_A_LAYERS, n_pool, PAGE_SIZE, KV_A, HEAD_DIM)),
        'cache_dsa_v': nrm(ks[3], (N_A_LAYERS, n_pool, PAGE_SIZE, KV_A, HEAD_DIM)),
        'cache_dsa_kidx': nrm(ks[4], (N_A_LAYERS, n_pool, PAGE_SIZE, D_IDX)),
        'cache_sb_k': nrm(ks[5], (N_B_LAYERS, n_pool, PAGE_SIZE, H_B, HEAD_DIM)),
        'cache_sb_v': nrm(ks[6], (N_B_LAYERS, n_pool, PAGE_SIZE, H_B, HEAD_DIM)),
        'state_conv': nrm(ks[8], (DEPTH, DEC_BATCH, CONV_W - 1, 2 * D_FF)),
        'page_table': page_table,
        'meta_tokens': nrm(ks[9], (N_META, D_MODEL)),
        'rel_bias_table': nrm(ks[10], (N_BUCKETS, H_A), 0.2),
        'w_in_a': nrm(ks[11], (N_A_LAYERS, D_MODEL, A_IN), D_MODEL ** -0.5),
        'w_out_a': nrm(ks[12], (N_A_LAYERS, H_A * HEAD_DIM, D_MODEL), (H_A * HEAD_DIM) ** -0.5),
        'w_in_b': nrm(ks[13], (N_B_LAYERS, D_MODEL, B_IN), D_MODEL ** -0.5),
        'w_out_b': nrm(ks[14], (N_B_LAYERS, H_B * HEAD_DIM, D_MODEL), (H_B * HEAD_DIM) ** -0.5),
        'norm_mix': 1.0 + nrm(ks[15], (DEPTH, D_MODEL), 0.02),
        'norm_ffn': 1.0 + nrm(ks[16], (DEPTH, D_MODEL), 0.02),
        'w_ffn_in': nrm(ks[17], (DEPTH, D_MODEL, 2 * D_FF), D_MODEL ** -0.5),
        'conv_w': nrm(ks[18], (DEPTH, CONV_W, 2 * D_FF), CONV_W ** -0.5),
        'conv_b': nrm(ks[19], (DEPTH, 2 * D_FF), 0.02),
        'w_ffn_out': nrm(ks[20], (DEPTH, D_FF, D_MODEL), D_FF ** -0.5),
        'norm_final': 1.0 + nrm(ks[21], (D_MODEL,), 0.02),
    }


def reference(x_prompt, x_sample, cache_dsa_k, cache_dsa_v, cache_dsa_kidx, cache_sb_k, cache_sb_v,
              state_conv, page_table, meta_tokens, rel_bias_table, w_in_a, w_out_a, w_in_b, w_out_b,
              norm_mix, norm_ffn, w_ffn_in, conv_w, conv_b, w_ffn_out, norm_final):
    Bn, S_p, D = x_prompt.shape
    DB, S_s, _ = x_sample.shape
    past_len = page_table.shape[1] * PAGE_SIZE
    pad = BLOCK - N_META
    T = pad + N_META + S_p
    xp = jnp.concatenate([jnp.zeros((Bn, pad, D), x_prompt.dtype),
                          jnp.broadcast_to(meta_tokens.astype(x_prompt.dtype)[None], (Bn, N_META, D)),
                          x_prompt], axis=1)
    pos = jnp.arange(T) - pad
    valid = pos >= 0
    xs = x_sample
    k_top_p = min(K_TOP_MAX, S_p // 4)
    k_top_s = min(K_TOP_MAX, (past_len + S_s) // 4)
    ones_s = jnp.ones((S_s,), dtype=bool)

    dk_p, dv_p, di_p, dk_s, dv_s, di_s = [], [], [], [], [], []
    sk_p, sv_p, sk_s, sv_s = [], [], [], []
    cv_p, cv_s = [], []
    for layer in range(DEPTH):
        li = layer // N_MIXERS
        hp = _rms_norm(xp, norm_mix[layer])
        hs = _rms_norm(xs, norm_mix[layer])
        if layer % N_MIXERS == 0:
            op, kp, vp, kip = _dsa_prompt(hp, w_in_a[li], w_out_a[li], rel_bias_table, pos, valid, k_top_p)
            os_, ks_, vs_, kis = _dsa_sample(hs, w_in_a[li], w_out_a[li], rel_bias_table, cache_dsa_k,
                                             cache_dsa_v, cache_dsa_kidx, li, page_table, k_top_s)
            dk_p.append(kp[:, pad:]); dv_p.append(vp[:, pad:]); di_p.append(kip[:, pad:])
            dk_s.append(ks_); dv_s.append(vs_); di_s.append(kis)
        else:
            op, kp, vp = _sb_prompt(hp, w_in_b[li], w_out_b[li], pos, valid)
            os_, ks_, vs_ = _sb_sample(hs, w_in_b[li], w_out_b[li], cache_sb_k, cache_sb_v, li, page_table)
            sk_p.append(kp[:, pad:]); sv_p.append(vp[:, pad:])
            sk_s.append(ks_); sv_s.append(vs_)
        xp = xp + op
        xs = xs + os_
        fp, stp = _conv_ffn(_rms_norm(xp, norm_ffn[layer]), jnp.zeros((Bn, CONV_W - 1, 2 * D_FF), xp.dtype),
                            valid, w_ffn_in[layer], conv_w[layer], conv_b[layer], w_ffn_out[layer])
        fs, sts = _conv_ffn(_rms_norm(xs, norm_ffn[layer]), state_conv[layer], ones_s,
                            w_ffn_in[layer], conv_w[layer], conv_b[layer], w_ffn_out[layer])
        xp = xp + fp
        xs = xs + fs
        cv_p.append(stp); cv_s.append(sts)

    y_prompt = _rms_norm(xp, norm_final)[:, pad + N_META:]
    y_sample = _rms_norm(xs, norm_final)
    return (y_prompt, y_sample,
            jnp.stack(dk_p), jnp.stack(dv_p), jnp.stack(di_p),
            jnp.stack(dk_s), jnp.stack(dv_s), jnp.stack(di_s),
            jnp.stack(sk_p), jnp.stack(sv_p), jnp.stack(sk_s), jnp.stack(sv_s),
            jnp.stack(cv_p), jnp.stack(cv_s))
```

```python
import functools
import math

import numpy as np
import jax
import jax.numpy as jnp
from jax import lax
from jax.experimental import pallas as pl
from jax.experimental.pallas import tpu as pltpu

F32 = jnp.float32
BF16 = jnp.bfloat16
I32 = jnp.int32

HEAD_DIM = 64
N_KV_A = 4
N_GROUP_A = 4
N_IDX_HEADS = 8
D_IDX = 64
K_TOP_MAX = 256
N_BUCKETS = 32
MAX_DISTANCE = 128
CONV_W = 3
EPS = 1e-6
PAGE = 128

ROW_TILE = 256
Q_BLOCK = 128
KEY_CHUNK = 256
FF_CHUNK = 256
VMEM_LIMIT = 56 * 1024 * 1024

NEG = -1e30
INT_MIN = -2 ** 31
NEG_INF_KEY = -2139095041
SB_DEAD = -104.0

_NT = (((1,), (1,)), ((), ()))


def _dot(a, b):
    return jnp.dot(a, b, preferred_element_type=F32)


def _dot_nt(a, b):
    return lax.dot_general(a, b, _NT, preferred_element_type=F32)


def _rms(x, g):
    y = x * lax.rsqrt(jnp.mean(x * x, axis=-1, keepdims=True) + EPS)
    return y * g


def _params(sem):
    return pltpu.CompilerParams(dimension_semantics=sem, vmem_limit_bytes=VMEM_LIMIT)


def _proj_kernel(x_ref, g_ref, w_ref, *o_refs, plan):
    h = _rms(x_ref[0], g_ref[...]).astype(BF16)
    for c0, cn, outs in plan:
        y = _dot(h, w_ref[:, c0:c0 + cn])
        for oi, scale, s0, sn in outs:
            v = y[:, s0:s0 + sn]
            if scale != 1.0:
                v = v * scale
            o_refs[oi][0] = v.astype(o_refs[oi].dtype)


def _norm_proj(x, g, w, plan, out_defs, tm):
    B, T, D = x.shape
    N = w.shape[1]
    out_shape = [jax.ShapeDtypeStruct((B, r, c), dt) for r, c, dt in out_defs]
    out_specs = [pl.BlockSpec((1, tm, c), lambda b, i: (b, i, 0)) for _, c, _ in out_defs]
    return pl.pallas_call(
        functools.partial(_proj_kernel, plan=plan),
        grid=(B, T // tm),
        in_specs=[pl.BlockSpec((1, tm, D), lambda b, i: (b, i, 0)),
                  pl.BlockSpec((1, D), lambda b, i: (0, 0)),
                  pl.BlockSpec((D, N), lambda b, i: (0, 0))],
        out_specs=out_specs,
        out_shape=out_shape,
        compiler_params=_params(("parallel", "arbitrary")),
        name="norm_proj",
    )(x, g.reshape(1, D), w)


def _ffn_prompt_kernel(x_ref, o_ref, wo_ref, g_ref, win_ref, cw_ref, cb_ref, wout_ref,
                       y_ref, st_ref, h_scr, x1_scr, acc_scr, carry_scr, *, n_chunks, last_row):
    i = pl.program_id(1)
    tm = x_ref.shape[1]

    @pl.when(i == 0)
    def _():
        carry_scr[...] = jnp.zeros_like(carry_scr)

    x1 = x_ref[0] + _dot(o_ref[0], wo_ref[...])
    x1_scr[...] = x1
    h_scr[...] = _rms(x1, g_ref[...]).astype(BF16)
    acc_scr[...] = jnp.zeros_like(acc_scr)
    rows = lax.broadcasted_iota(I32, (tm, FF_CHUNK), 0)
    t_last, r_last = divmod(last_row, tm)
    assert r_last >= CONV_W - 2

    def conv(j, u):
        prev = carry_scr[j]
        u1 = jnp.where(rows >= 1, pltpu.roll(u, 1, 0), prev[7:8])
        u2 = jnp.where(rows >= 2, pltpu.roll(u, 2, 0), jnp.where(rows == 0, prev[6:7], prev[7:8]))
        carry_scr[j] = u[tm - 8:tm]
        cw = cw_ref[j]

        @pl.when(i == t_last)
        def _():
            st_ref[0, j] = u[r_last - 1:r_last + 1]
        return cb_ref[j] + cw[0:1] * u2 + cw[1:2] * u1 + cw[2:3] * u

    def body(j, _):
        h = h_scr[...]
        cg = conv(j, _dot(h, win_ref[j]))
        cv = conv(j + n_chunks, _dot(h, win_ref[j + n_chunks]))
        act = (cg * jax.nn.sigmoid(cg)) * cv
        acc_scr[...] += _dot(act.astype(BF16), wout_ref[j])
        return 0

    lax.fori_loop(0, n_chunks, body, 0)
    y_ref[0] = x1_scr[...] + acc_scr[...]


def _ffn_prompt(x, o, wo, g, win_c, cw_c, cb_c, wout_c, last_row, tm):
    B, T, D = x.shape
    n2, _, fc = win_c.shape
    n_chunks = n2 // 2
    F2 = n2 * fc
    const = lambda *shape: pl.BlockSpec(shape, lambda b, i: (0,) * len(shape))
    return pl.pallas_call(
        functools.partial(_ffn_prompt_kernel, n_chunks=n_chunks, last_row=last_row),
        grid=(B, T // tm),
        in_specs=[pl.BlockSpec((1, tm, D), lambda b, i: (b, i, 0)),
                  pl.BlockSpec((1, tm, o.shape[2]), lambda b, i: (b, i, 0)),
                  const(*wo.shape), const(1, D), const(*win_c.shape), const(*cw_c.shape),
                  const(*cb_c.shape), const(*wout_c.shape)],
        out_specs=[pl.BlockSpec((1, tm, D), lambda b, i: (b, i, 0)),
                   pl.BlockSpec((1, n2, CONV_W - 1, fc), lambda b, i: (b, 0, 0, 0))],
        out_shape=[jax.ShapeDtypeStruct((B, T, D), F32),
                   jax.ShapeDtypeStruct((B, n2, CONV_W - 1, fc), F32)],
        scratch_shapes=[pltpu.VMEM((tm, D), BF16), pltpu.VMEM((tm, D), F32),
                        pltpu.VMEM((tm, D), F32), pltpu.VMEM((n2, 8, fc), F32)],
        compiler_params=_params(("parallel", "arbitrary")),
        name="ffn_prompt",
    )(x, o, wo, g.reshape(1, D), win_c, cw_c, cb_c, wout_c)


def _ffn_sample_kernel(x_ref, o_ref, wo_ref, g_ref, win_g_ref, win_v_ref, cw_g_ref, cw_v_ref,
                       cb_g_ref, cb_v_ref, p0g_ref, p1g_ref, p0v_ref, p1v_ref, wout_ref,
                       y_ref, ug_ref, uv_ref, h_scr, x1_scr, acc_scr):
    j = pl.program_id(0)

    @pl.when(j == 0)
    def _():
        x1 = x_ref[...] + _dot(o_ref[...], wo_ref[...])
        x1_scr[...] = x1
        h_scr[...] = _rms(x1, g_ref[...]).astype(BF16)
        acc_scr[...] = jnp.zeros_like(acc_scr)

    h = h_scr[...]
    ug = _dot(h, win_g_ref[0])
    uv = _dot(h, win_v_ref[0])
    ug_ref[...] = ug
    uv_ref[...] = uv
    cwg = cw_g_ref[0]
    cwv = cw_v_ref[0]
    cg = cb_g_ref[0] + cwg[0:1] * p0g_ref[...] + cwg[1:2] * p1g_ref[...] + cwg[2:3] * ug
    cv = cb_v_ref[0] + cwv[0:1] * p0v_ref[...] + cwv[1:2] * p1v_ref[...] + cwv[2:3] * uv
    act = (cg * jax.nn.sigmoid(cg)) * cv
    acc_scr[...] += _dot(act.astype(BF16), wout_ref[0])

    @pl.when(j == pl.num_programs(0) - 1)
    def _():
        y_ref[...] = x1_scr[...] + acc_scr[...]


def _ffn_sample(x, o, wo, g, win_c, cw_c, cb_c, wout_c, prev0, prev1):
    M, D = x.shape
    n2, _, fc = win_c.shape
    nc = n2 // 2
    F2 = n2 * fc
    const = lambda *shape: pl.BlockSpec(shape, lambda j: (0,) * len(shape))
    gsel = lambda j: (j, 0, 0)
    vsel = lambda j: (j + nc, 0, 0)
    y, ug, uv = pl.pallas_call(
        _ffn_sample_kernel,
        grid=(nc,),
        in_specs=[const(M, D), const(M, o.shape[1]), const(*wo.shape), const(1, D),
                  pl.BlockSpec((1, D, fc), gsel), pl.BlockSpec((1, D, fc), vsel),
                  pl.BlockSpec((1, CONV_W, fc), gsel), pl.BlockSpec((1, CONV_W, fc), vsel),
                  pl.BlockSpec((1, 1, fc), gsel), pl.BlockSpec((1, 1, fc), vsel),
                  pl.BlockSpec((M, fc), lambda j: (0, j)), pl.BlockSpec((M, fc), lambda j: (0, j)),
                  pl.BlockSpec((M, fc), lambda j: (0, j + nc)), pl.BlockSpec((M, fc), lambda j: (0, j + nc)),
                  pl.BlockSpec((1, fc, D), gsel)],
        out_specs=[const(M, D), pl.BlockSpec((M, fc), lambda j: (0, j)),
                   pl.BlockSpec((M, fc), lambda j: (0, j))],
        out_shape=[jax.ShapeDtypeStruct((M, D), F32), jax.ShapeDtypeStruct((M, F2 // 2), F32),
                   jax.ShapeDtypeStruct((M, F2 // 2), F32)],
        scratch_shapes=[pltpu.VMEM((M, D), BF16), pltpu.VMEM((M, D), F32), pltpu.VMEM((M, D), F32)],
        compiler_params=_params(("arbitrary",)),
        name="ffn_sample",
    )(x, o, wo, g.reshape(1, D), win_c, win_c, cw_c, cw_c, cb_c, cb_c,
      prev0, prev1, prev0, prev1, wout_c)
    return y, jnp.concatenate([ug, uv], axis=-1)


def _sortable(score):
    score = jnp.where(score == 0.0, 0.0, score)
    bits = lax.bitcast_convert_type(score, I32)
    return jnp.where(bits < 0, bits ^ jnp.int32(0x7FFFFFFF), bits)


def _select_mask(key_scr, m_scr, nch, k_top, rows):
    W = key_scr.shape[2]
    col = lax.broadcasted_iota(I32, (rows, W), 1)

    def count(pred):
        def body(c, acc):
            return acc + pred(key_scr[c], c).astype(I32)
        acc = lax.fori_loop(0, nch, body, jnp.zeros((rows, W), I32))
        return jnp.sum(acc, axis=1, keepdims=True)

    def count_ge(cand):
        return count(lambda k, c: k >= cand)

    zero = jnp.zeros((rows, 1), I32)
    lo = jnp.where(count_ge(zero) >= k_top, zero, jnp.int32(INT_MIN))

    def bit_body(t, lo):
        cand = lo + jnp.left_shift(jnp.int32(1), 30 - t)
        return jnp.where(count_ge(cand) >= k_top, cand, lo)

    thr = lax.fori_loop(0, 31, bit_body, lo)
    n_gt = count_ge(thr + 1)
    need = k_top - n_gt
    n_eq = count_ge(thr) - n_gt
    tied = jnp.max(jnp.where((n_eq > need) & (thr > NEG_INF_KEY), 1, 0)) > 0
    big = jnp.full((rows, 1), 2 ** 30, I32)

    def find_cut():
        def body(t, ans):
            cand = ans + jnp.left_shift(jnp.int32(1), 14 - t)
            cnt = count(lambda k, c: (k == thr) & (col + c * W < cand))
            return jnp.where(cnt < need, cand, ans)
        return lax.fori_loop(0, 15, body, zero)

    cut = lax.cond(tied, find_cut, lambda: big)

    def mask_body(c, _):
        k = key_scr[c]
        sel = (k > thr) | ((k == thr) & (col + c * W <= cut))
        sel = sel & (k > NEG_INF_KEY)
        m_scr[c] = jnp.where(sel, 0.0, NEG)
        return 0

    lax.fori_loop(0, nch, mask_body, 0)


def _dsa_prompt_kernel(qi_ref, misc_ref, q_ref, kib_ref, kb_ref, vb_ref, nb_ref, o_ref,
                       key_scr, m_scr, *, k_top):
    i = pl.program_id(1)
    QB, W = Q_BLOCK, KEY_CHUNK
    nch = (i + 2) // 2
    nfar = jnp.maximum((i - 1) // 2, 0)

    qi = qi_ref[0]
    wi = misc_ref[0][:, D_IDX:D_IDX + N_IDX_HEADS]
    qis = jnp.concatenate([qi[:, D_IDX * h:D_IDX * (h + 1)] for h in range(N_IDX_HEADS)], axis=0)
    row = lax.broadcasted_iota(I32, (QB, W), 0) + i * QB
    col = lax.broadcasted_iota(I32, (QB, W), 1)

    def score_body(c, _):
        k0 = pl.multiple_of(c * W, W)
        s = _dot_nt(qis, kib_ref[0, pl.ds(k0, W), :])
        acc = jnp.zeros((QB, W), F32)
        for h in range(N_IDX_HEADS):
            acc = acc + jnp.maximum(s[QB * h:QB * (h + 1)] * (D_IDX ** -0.5), 0.0) * wi[:, h:h + 1]
        score = acc * (N_IDX_HEADS ** -0.5)
        score = jnp.where(col + k0 <= row, score, -jnp.inf)
        key_scr[c] = _sortable(score)
        return 0

    lax.fori_loop(0, nch, score_body, 0)
    _select_mask(key_scr, m_scr, nch, k_top, QB)

    q = q_ref[0]
    G = N_GROUP_A
    for ckv in range(N_KV_A):
        heads = [ckv * G + g for g in range(G)]
        qc = jnp.concatenate([q[:, HEAD_DIM * h:HEAD_DIM * (h + 1)] for h in heads], axis=0)
        lanes = slice(HEAD_DIM * ckv, HEAD_DIM * (ckv + 1))

        def att_body(c, carry, near):
            m, l, acc = carry
            k0 = pl.multiple_of(c * W, W)
            s = _dot_nt(qc, kb_ref[0, pl.ds(k0, W), lanes]).reshape(G, QB, W)
            s = s + m_scr[c][None]
            if near:
                halves = []
                for hf in range(2):
                    jb = 2 * c + hf
                    idx = jnp.where(jb == i, 0, jnp.where(jb == i - 1, 1, 2))
                    halves.append(jnp.stack([nb_ref[idx, h] for h in heads], axis=0))
                s = s + jnp.concatenate(halves, axis=-1)
            m_new = jnp.maximum(m, jnp.max(s, axis=-1, keepdims=True))
            alpha = jnp.exp(m - m_new)
            p = jnp.exp(s - m_new)
            l = alpha * l + jnp.sum(p, axis=-1, keepdims=True)
            pv = _dot(p.reshape(G * QB, W).astype(BF16), vb_ref[0, pl.ds(k0, W), lanes])
            acc = alpha * acc + pv.reshape(G, QB, HEAD_DIM)
            return m_new, l, acc

        carry = (jnp.full((G, QB, 1), NEG, F32), jnp.zeros((G, QB, 1), F32),
                 jnp.zeros((G, QB, HEAD_DIM), F32))
        carry = lax.fori_loop(0, nfar, functools.partial(att_body, near=False), carry)
        m, l, acc = lax.fori_loop(nfar, nch, functools.partial(att_body, near=True), carry)
        out = acc / l
        for g, h in enumerate(heads):
            o_ref[0, :, HEAD_DIM * h:HEAD_DIM * (h + 1)] = out[g].astype(o_ref.dtype)


def _dsa_prompt(qi, misc, q, kib, kb, vb, near_bias, k_top):
    B, T, _ = q.shape
    nq = T // Q_BLOCK
    nc = T // KEY_CHUNK
    blk = lambda c: pl.BlockSpec((1, Q_BLOCK, c), lambda b, i: (b, i, 0))
    res = lambda c: pl.BlockSpec((1, T, c), lambda b, i: (b, 0, 0))
    return pl.pallas_call(
        functools.partial(_dsa_prompt_kernel, k_top=k_top),
        grid=(B, nq),
        in_specs=[blk(qi.shape[2]), blk(misc.shape[2]), blk(q.shape[2]),
                  res(kib.shape[2]), res(kb.shape[2]), res(vb.shape[2]),
                  pl.BlockSpec(near_bias.shape, lambda b, i: (0, 0, 0, 0))],
        out_specs=blk(q.shape[2]),
        out_shape=jax.ShapeDtypeStruct(q.shape, BF16),
        scratch_shapes=[pltpu.VMEM((nc, Q_BLOCK, KEY_CHUNK), I32),
                        pltpu.VMEM((nc, Q_BLOCK, KEY_CHUNK), F32)],
        compiler_params=_params(("parallel", "arbitrary")),
        name="dsa_prompt",
    )(qi, misc, q, kib, kb, vb, near_bias)


def _log_sigmoid_pair(z):
    t = jnp.log1p(jnp.exp(-jnp.abs(z)))
    return -(jnp.maximum(z, 0.0) + t), jnp.minimum(z, 0.0) - t


def _suffix_in_chunk(ln, u):
    hi = ln.astype(BF16)
    lo = (ln - hi.astype(F32)).astype(BF16)
    return _dot(hi, u) + _dot(lo, u)


def _sb_prompt_kernel(q_ref, kb_ref, vb_ref, u_ref, o_ref):
    i = pl.program_id(2)
    QB, W = Q_BLOCK, KEY_CHUNK
    cd = i // 2
    q = q_ref[0]
    u = u_ref[...]
    row = lax.broadcasted_iota(I32, (QB, W), 0) + i * QB
    col = lax.broadcasted_iota(I32, (QB, W), 1)
    n_heads = q.shape[1] // HEAD_DIM

    def chunk(c, rs, accs, masked):
        k0 = pl.multiple_of(c * W, W)
        kc = kb_ref[0, pl.ds(k0, W), :]
        vc = vb_ref[0, pl.ds(k0, W), :]
        if masked:
            valid = col + k0 < row
        rs_out, accs_out = [], []
        for hh in range(n_heads):
            lanes = slice(HEAD_DIM * hh, HEAD_DIM * (hh + 1))
            z = _dot_nt(q[:, lanes], kc[:, lanes])
            ln, lp = _log_sigmoid_pair(z)
            if masked:
                ln = jnp.where(valid, ln, 0.0)
            a = jnp.exp(lp + _suffix_in_chunk(ln, u) + rs[hh])
            if masked:
                a = jnp.where(valid, a, 0.0)
            accs_out.append(accs[hh] + _dot(a.astype(BF16), vc[:, lanes]))
            rs_out.append(rs[hh] + jnp.sum(ln, axis=-1, keepdims=True))
        return tuple(rs_out), tuple(accs_out)

    rs, accs = chunk(cd, (jnp.zeros((QB, 1), F32),) * n_heads,
                     (jnp.zeros((QB, HEAD_DIM), F32),) * n_heads, True)

    def cond(state):
        c, rs, _ = state
        return (c >= 0) & (jnp.max(functools.reduce(jnp.maximum, rs)) > SB_DEAD)

    def body(state):
        c, rs, accs = state
        rs, accs = chunk(c, rs, accs, False)
        return c - 1, rs, accs

    _, _, accs = lax.while_loop(cond, body, (cd - 1, rs, accs))
    o_ref[0] = jnp.concatenate(accs, axis=-1).astype(o_ref.dtype)


def _sb_prompt(q, kb, vb, u, heads_per_step=4):
    B, T, HD = q.shape
    gw = heads_per_step * HEAD_DIM
    return pl.pallas_call(
        _sb_prompt_kernel,
        grid=(B, HD // gw, T // Q_BLOCK),
        in_specs=[pl.BlockSpec((1, Q_BLOCK, gw), lambda b, g, i: (b, i, g)),
                  pl.BlockSpec((1, T, gw), lambda b, g, i: (b, 0, g)),
                  pl.BlockSpec((1, T, gw), lambda b, g, i: (b, 0, g)),
                  pl.BlockSpec(u.shape, lambda b, g, i: (0, 0))],
        out_specs=pl.BlockSpec((1, Q_BLOCK, gw), lambda b, g, i: (b, i, g)),
        out_shape=jax.ShapeDtypeStruct(q.shape, BF16),
        compiler_params=_params(("parallel", "parallel", "arbitrary")),
        name="sb_prompt",
    )(q, kb, vb, u)


def _dsa_sample_score_kernel(pt_ref, qi_ref, wi_ref, kin_ref, *rest, n_pages):
    pages = rest[:n_pages]
    o_ref = rest[n_pages]
    qi = qi_ref[0]
    wi = wi_ref[0]

    def score(kidx_bf16):
        s = _dot_nt(qi, kidx_bf16)
        s = jnp.maximum(s * (D_IDX ** -0.5), 0.0) * wi
        return jnp.sum(s, axis=0, keepdims=True) * (N_IDX_HEADS ** -0.5)

    for p in range(n_pages):
        o_ref[0, :, PAGE * p:PAGE * (p + 1)] = score(pages[p][0, 0].astype(BF16))
    tail = score(kin_ref[0])
    lane = lax.broadcasted_iota(I32, (1, PAGE), 1)
    o_ref[0, :, PAGE * n_pages:] = jnp.where(lane == 0, tail, -jnp.inf)


def _dsa_sample_scores(page_table, qi, wi, ki_new, cache_kidx, li):
    DB, n_pages = page_table.shape
    L = PAGE * (n_pages + 1)

    def page_spec(p):
        return pl.BlockSpec((1, 1, PAGE, D_IDX), lambda b, pt: (li, pt[b, p], 0, 0))

    grid_spec = pltpu.PrefetchScalarGridSpec(
        num_scalar_prefetch=1,
        grid=(DB,),
        in_specs=[pl.BlockSpec((1, N_IDX_HEADS, D_IDX), lambda b, pt: (b, 0, 0)),
                  pl.BlockSpec((1, N_IDX_HEADS, 1), lambda b, pt: (b, 0, 0)),
                  pl.BlockSpec((1, PAGE, D_IDX), lambda b, pt: (b, 0, 0))]
                 + [page_spec(p) for p in range(n_pages)],
        out_specs=pl.BlockSpec((1, 1, L), lambda b, pt: (b, 0, 0)),
    )
    return pl.pallas_call(
        functools.partial(_dsa_sample_score_kernel, n_pages=n_pages),
        grid_spec=grid_spec,
        out_shape=jax.ShapeDtypeStruct((DB, 1, L), F32),
        compiler_params=_params(("arbitrary",)),
        name="dsa_sample_scores",
    )(page_table, qi, wi, ki_new, *([cache_kidx] * n_pages))


def _dsa_sample_select_kernel(s_ref, m_ref, key_scr, *, k_top):
    nch = key_scr.shape[0]
    W = key_scr.shape[2]
    for c in range(nch):
        key_scr[c] = _sortable(s_ref[:, W * c:W * (c + 1)])
    _select_mask(key_scr, m_ref, nch, k_top, s_ref.shape[0])


def _dsa_sample_select(scores, k_top):
    DB, L = scores.shape
    nch = L // PAGE
    return pl.pallas_call(
        functools.partial(_dsa_sample_select_kernel, k_top=k_top),
        out_shape=jax.ShapeDtypeStruct((nch, DB, PAGE), F32),
        scratch_shapes=[pltpu.VMEM((nch, DB, PAGE), I32)],
        compiler_params=pltpu.CompilerParams(vmem_limit_bytes=VMEM_LIMIT),
        name="dsa_sample_select",
    )(scores)


def _dsa_sample_attn_kernel(pt_ref, q_ref, mask_ref, bias_ref, kn_ref, vn_ref, hm_ref, *rest, n_pages):
    kp = rest[:n_pages]
    vp = rest[n_pages:2 * n_pages]
    o_ref = rest[2 * n_pages]
    s_scr = rest[2 * n_pages + 1]
    H = q_ref.shape[1]
    q = q_ref[0]
    past = PAGE * n_pages
    for p in range(n_pages):
        s_scr[:, PAGE * p:PAGE * (p + 1)] = _dot_nt(q, kp[p][0, 0].astype(BF16))
    s_new = jnp.sum(q.astype(F32) * kn_ref[0].astype(BF16).astype(F32), axis=-1, keepdims=True)
    lane = lax.broadcasted_iota(I32, (H, PAGE), 1)
    s_scr[:, past:] = jnp.where(lane == 0, s_new, NEG)
    s = s_scr[...] + bias_ref[...] + mask_ref[0]
    m = jnp.max(s, axis=-1, keepdims=True)
    p_all = jnp.exp(s - m)
    l = jnp.sum(p_all, axis=-1, keepdims=True)
    acc = p_all[:, past:past + 1] * vn_ref[0]
    pb = p_all.astype(BF16)
    for p in range(n_pages):
        acc = acc + _dot(pb[:, PAGE * p:PAGE * (p + 1)], vp[p][0, 0].astype(BF16))
    out = acc * hm_ref[...] / l
    o = out[:, 0:HEAD_DIM]
    for c in range(1, N_KV_A):
        o = o + out[:, HEAD_DIM * c:HEAD_DIM * (c + 1)]
    o_ref[0] = o.astype(o_ref.dtype)


def _dsa_sample_attn(page_table, q_bd, mask, bias, k_new, v_new, head_mask, cache_k, cache_v, li):
    DB, n_pages = page_table.shape
    H = q_bd.shape[1]
    KW = q_bd.shape[2]
    L = PAGE * (n_pages + 1)

    def page_spec(p):
        return pl.BlockSpec((1, 1, PAGE, KW), lambda b, pt: (li, pt[b, p], 0, 0))

    grid_spec = pltpu.PrefetchScalarGridSpec(
        num_scalar_prefetch=1,
        grid=(DB,),
        in_specs=[pl.BlockSpec((1, H, KW), lambda b, pt: (b, 0, 0)),
                  pl.BlockSpec((1, 1, L), lambda b, pt: (b, 0, 0)),
                  pl.BlockSpec((H, L), lambda b, pt: (0, 0)),
                  pl.BlockSpec((1, 1, KW), lambda b, pt: (b, 0, 0)),
                  pl.BlockSpec((1, 1, KW), lambda b, pt: (b, 0, 0)),
                  pl.BlockSpec((H, KW), lambda b, pt: (0, 0))]
                 + [page_spec(p) for p in range(n_pages)] * 2,
        out_specs=pl.BlockSpec((1, H, HEAD_DIM), lambda b, pt: (b, 0, 0)),
        scratch_shapes=[pltpu.VMEM((H, L), F32)],
    )
    return pl.pallas_call(
        functools.partial(_dsa_sample_attn_kernel, n_pages=n_pages),
        grid_spec=grid_spec,
        out_shape=jax.ShapeDtypeStruct((DB, H, HEAD_DIM), BF16),
        compiler_params=_params(("arbitrary",)),
        name="dsa_sample_attn",
    )(page_table, q_bd, mask, bias, k_new, v_new, head_mask,
      *([cache_k] * n_pages), *([cache_v] * n_pages))


def _sb_sample_kernel(pt_ref, q_ref, u_ref, hm_ref, *rest, n_half):
    kp = rest[:n_half]
    vp = rest[n_half:2 * n_half]
    o_ref = rest[2 * n_half]
    r_scr, acc_scr = rest[2 * n_half + 1:]
    t = pl.program_id(1)

    @pl.when(t == 0)
    def _():
        r_scr[...] = jnp.zeros_like(r_scr)
        acc_scr[...] = jnp.zeros_like(acc_scr)

    q = q_ref[0]
    u = u_ref[...]
    r = r_scr[...]
    acc = acc_scr[...]
    for p in reversed(range(n_half)):
        z = _dot_nt(q, kp[p][0, 0].astype(BF16))
        ln, lp = _log_sigmoid_pair(z)
        a = jnp.exp(lp + _suffix_in_chunk(ln, u) + r)
        acc = acc + _dot(a.astype(BF16), vp[p][0, 0].astype(BF16))
        r = r + jnp.sum(ln, axis=-1, keepdims=True)
    r_scr[...] = r
    acc_scr[...] = acc

    @pl.when(t == pl.num_programs(1) - 1)
    def _():
        o_ref[0] = jnp.sum(acc * hm_ref[...], axis=0, keepdims=True).astype(o_ref.dtype)


def _sb_sample(page_table, q_bd, u, head_mask, cache_k, cache_v, li, n_split):
    DB, n_pages = page_table.shape
    H, KW = q_bd.shape[1], q_bd.shape[2]
    n_half = n_pages // n_split

    def page_spec(p):
        return pl.BlockSpec((1, 1, PAGE, KW),
                            lambda b, t, pt: (li, pt[b, (n_split - 1 - t) * n_half + p], 0, 0))

    grid_spec = pltpu.PrefetchScalarGridSpec(
        num_scalar_prefetch=1,
        grid=(DB, n_split),
        in_specs=[pl.BlockSpec((1, H, KW), lambda b, t, pt: (b, 0, 0)),
                  pl.BlockSpec(u.shape, lambda b, t, pt: (0, 0)),
                  pl.BlockSpec((H, KW), lambda b, t, pt: (0, 0))]
                 + [page_spec(p) for p in range(n_half)] * 2,
        out_specs=pl.BlockSpec((1, 1, KW), lambda b, t, pt: (b, 0, 0)),
        scratch_shapes=[pltpu.VMEM((H, 1), F32), pltpu.VMEM((H, KW), F32)],
    )
    return pl.pallas_call(
        functools.partial(_sb_sample_kernel, n_half=n_half),
        grid_spec=grid_spec,
        out_shape=jax.ShapeDtypeStruct((DB, 1, KW), BF16),
        compiler_params=_params(("arbitrary", "arbitrary")),
        name="sb_sample",
    )(page_table, q_bd, u, head_mask, *([cache_k] * n_half), *([cache_v] * n_half))


def _final_norm_kernel(xa_ref, xb_ref, g_ref, o_ref, *, row0):
    if row0:
        x = jnp.concatenate([xa_ref[0][row0:], xb_ref[0][:row0]], axis=0)
    else:
        x = xa_ref[0]
    o_ref[0] = _rms(x, g_ref[...])


def _final_norm(x, g, row0, rows, tm):
    B, T, D = x.shape
    assert row0 % 8 == 0 and row0 < tm and rows % tm == 0 and (row0 == 0 or rows + tm <= T)
    nxt = (lambda b, i: (b, i + 1, 0)) if row0 else (lambda b, i: (b, i, 0))
    return pl.pallas_call(
        functools.partial(_final_norm_kernel, row0=row0),
        grid=(B, rows // tm),
        in_specs=[pl.BlockSpec((1, tm, D), lambda b, i: (b, i, 0)),
                  pl.BlockSpec((1, tm, D), nxt),
                  pl.BlockSpec((1, D), lambda b, i: (0, 0))],
        out_specs=pl.BlockSpec((1, tm, D), lambda b, i: (b, i, 0)),
        out_shape=jax.ShapeDtypeStruct((B, rows, D), F32),
        compiler_params=_params(("parallel", "parallel")),
        name="final_norm",
    )(x, x, g.reshape(1, D))


def _t5_bucket(dist):
    max_exact = N_BUCKETS // 2
    d = jnp.maximum(dist, 1).astype(F32)
    large = max_exact + (jnp.log(d / max_exact) / math.log(MAX_DISTANCE / max_exact)
                         * (N_BUCKETS - max_exact)).astype(I32)
    return jnp.where(dist < max_exact, dist, jnp.minimum(large, N_BUCKETS - 1))


def _chunk_cols(w, fc):
    K, N = w.shape
    return w.reshape(K, N // fc, fc).transpose(1, 0, 2)


def kernel(x_prompt, x_sample, cache_dsa_k, cache_dsa_v, cache_dsa_kidx, cache_sb_k, cache_sb_v,
           state_conv, page_table, meta_tokens, rel_bias_table, w_in_a, w_out_a, w_in_b, w_out_b,
           norm_mix, norm_ffn, w_ffn_in, conv_w, conv_b, w_ffn_out, norm_final):
    B, S_p, D = x_prompt.shape
    DB, S_s, _ = x_sample.shape
    assert S_s == 1
    depth = norm_mix.shape[0]
    n_meta = meta_tokens.shape[0]
    n_pages = page_table.shape[1]
    past = n_pages * PAGE
    H_A = N_KV_A * N_GROUP_A
    H_B = w_in_b.shape[2] // (3 * HEAD_DIM)
    QW = H_A * HEAD_DIM
    KW = N_KV_A * HEAD_DIM
    SBW = H_B * HEAD_DIM
    F2 = w_ffn_in.shape[2]
    k_top_p = min(K_TOP_MAX, S_p // 4)
    k_top_s = min(K_TOP_MAX, (past + S_s) // 4)
    TV = n_meta + S_p
    T = -(-TV // ROW_TILE) * ROW_TILE
    tm = ROW_TILE
    scale = HEAD_DIM ** -0.5

    xp = jnp.concatenate([jnp.broadcast_to(meta_tokens[None], (B, n_meta, D)), x_prompt,
                          jnp.zeros((B, T - TV, D), x_prompt.dtype)], axis=1)
    xs = x_sample.reshape(1, DB, D)

    dist = jnp.arange(2 * Q_BLOCK)
    bias_d = (rel_bias_table[_t5_bucket(dist)] - rel_bias_table[N_BUCKETS - 1]).T
    rq = np.arange(Q_BLOCK)[:, None] - np.arange(Q_BLOCK)[None, :]
    near = jnp.stack([bias_d[:, np.clip(rq, 0, None)], bias_d[:, rq + Q_BLOCK],
                      jnp.zeros((H_A, Q_BLOCK, Q_BLOCK), F32)], axis=0)
    d_s = past - jnp.arange(past + PAGE)
    bias_s = (rel_bias_table[_t5_bucket(jnp.maximum(d_s, 0))] - rel_bias_table[N_BUCKETS - 1]).T
    u_tri = jnp.asarray(np.tril(np.ones((KEY_CHUNK, KEY_CHUNK), np.float32), -1), BF16)
    u_page = u_tri[:PAGE, :PAGE]
    hm_a = jnp.asarray((np.arange(H_A)[:, None] // N_GROUP_A) == (np.arange(KW)[None, :] // HEAD_DIM), F32)
    hm_b = jnp.asarray(np.arange(H_B)[:, None] == (np.arange(SBW)[None, :] // HEAD_DIM), F32)

    pad_a = -(-w_in_a.shape[2] // 128) * 128 - w_in_a.shape[2]
    misc0 = QW + 2 * KW + N_IDX_HEADS * D_IDX
    plan_a = [(0, QW, [(0, scale, 0, QW)]),
              (QW, KW, [(1, 1.0, 0, KW), (3, 1.0, 0, KW)]),
              (QW + KW, KW, [(2, 1.0, 0, KW), (4, 1.0, 0, KW)]),
              (QW + 2 * KW, N_IDX_HEADS * D_IDX, [(5, 1.0, 0, N_IDX_HEADS * D_IDX)]),
              (misc0, 128, [(6, 1.0, 0, 128), (7, 1.0, 0, D_IDX), (8, 1.0, 0, D_IDX)])]
    plan_b = [(0, SBW, [(0, scale, 0, SBW)]),
              (SBW, SBW, [(1, 1.0, 0, SBW), (3, 1.0, 0, SBW)]),
              (2 * SBW, SBW, [(2, 1.0, 0, SBW), (4, 1.0, 0, SBW)])]

    def defs_a(rows_all, rows_out):
        return [(rows_all, QW, BF16), (rows_out, KW, F32), (rows_out, KW, F32), (rows_all, KW, BF16),
                (rows_all, KW, BF16), (rows_all, N_IDX_HEADS * D_IDX, BF16), (rows_all, 128, F32),
                (rows_out, D_IDX, F32), (rows_all, D_IDX, BF16)]

    def defs_b(rows_all, rows_out):
        return [(rows_all, SBW, BF16), (rows_out, SBW, F32), (rows_out, SBW, F32),
                (rows_all, SBW, BF16), (rows_all, SBW, BF16)]

    dk_p, dv_p, di_p, dk_s, dv_s, di_s = [], [], [], [], [], []
    sk_p, sv_p, sk_s, sv_s = [], [], [], []
    cv_p, cv_s = [], []
    for layer in range(depth):
        li = layer // 2
        if layer % 2 == 0:
            w = jnp.pad(w_in_a[li], ((0, 0), (0, pad_a))).astype(BF16)
            wo = w_out_a[li].astype(BF16)
            q, k, v, kb, vb, qi, misc, ki, kib = _norm_proj(xp, norm_mix[layer], w, plan_a, defs_a(T, TV), tm)
            op = _dsa_prompt(qi, misc, q, kib, kb, vb, near, k_top_p)
            dk_p.append(k.reshape(B, TV, N_KV_A, HEAD_DIM))
            dv_p.append(v.reshape(B, TV, N_KV_A, HEAD_DIM))
            di_p.append(ki)

            qs, ks, vs, _, _, qis, miscs, kis, kibs = _norm_proj(
                xs, norm_mix[layer], w, plan_a, defs_a(DB, DB), DB)
            wis = miscs[0, :, D_IDX:D_IDX + N_IDX_HEADS]
            kin = jnp.pad(kibs[0][:, None, :], ((0, 0), (0, PAGE - 1), (0, 0)))
            scores = _dsa_sample_scores(page_table, qis[0].reshape(DB, N_IDX_HEADS, D_IDX),
                                        wis[:, :, None], kin, cache_dsa_kidx, li)
            mask = _dsa_sample_select(scores[:, 0], k_top_s)
            mask = mask.transpose(1, 0, 2).reshape(DB, 1, past + PAGE)
            q_bd = (qs[0].reshape(DB, H_A, 1, HEAD_DIM) * hm_a.reshape(1, H_A, N_KV_A, HEAD_DIM).astype(BF16))
            os_ = _dsa_sample_attn(page_table, q_bd.reshape(DB, H_A, KW), mask, bias_s,
                                   ks[0][:, None, :], vs[0][:, None, :], hm_a,
                                   cache_dsa_k.reshape(cache_dsa_k.shape[:3] + (KW,)),
                                   cache_dsa_v.reshape(cache_dsa_v.shape[:3] + (KW,)), li)
            os_ = os_.reshape(DB, QW)
            dk_s.append(ks[0].reshape(DB, S_s, N_KV_A, HEAD_DIM))
            dv_s.append(vs[0].reshape(DB, S_s, N_KV_A, HEAD_DIM))
            di_s.append(kis[0].reshape(DB, S_s, D_IDX))
        else:
            w = w_in_b[li].astype(BF16)
            wo = w_out_b[li].astype(BF16)
            q, k, v, kb, vb = _norm_proj(xp, norm_mix[layer], w, plan_b, defs_b(T, TV), tm)
            op = _sb_prompt(q, kb, vb, u_tri)
            sk_p.append(k.reshape(B, TV, H_B, HEAD_DIM))
            sv_p.append(v.reshape(B, TV, H_B, HEAD_DIM))

            qs, ks, vs, _, _ = _norm_proj(xs, norm_mix[layer], w, plan_b, defs_b(DB, DB), DB)
            q_bd = (qs[0].reshape(DB, H_B, 1, HEAD_DIM) * hm_b.reshape(1, H_B, H_B, HEAD_DIM).astype(BF16))
            os_ = _sb_sample(page_table, q_bd.reshape(DB, H_B, SBW), u_page, hm_b,
                             cache_sb_k.reshape(cache_sb_k.shape[:3] + (SBW,)),
                             cache_sb_v.reshape(cache_sb_v.shape[:3] + (SBW,)), li, 2)
            os_ = os_.reshape(DB, SBW)
            sk_s.append(ks[0].reshape(DB, S_s, H_B, HEAD_DIM))
            sv_s.append(vs[0].reshape(DB, S_s, H_B, HEAD_DIM))

        win_c = _chunk_cols(w_ffn_in[layer].astype(BF16), FF_CHUNK)
        cw_c = conv_w[layer].reshape(CONV_W, F2 // FF_CHUNK, FF_CHUNK).transpose(1, 0, 2)
        cb_c = conv_b[layer].reshape(F2 // FF_CHUNK, 1, FF_CHUNK)
        wout_c = w_ffn_out[layer].astype(BF16).reshape(F2 // 2 // FF_CHUNK, FF_CHUNK, D)
        xp, stp = _ffn_prompt(xp, op, wo, norm_ffn[layer], win_c, cw_c, cb_c, wout_c, TV - 1, tm)
        ys, us = _ffn_sample(xs[0], os_, wo, norm_ffn[layer], win_c, cw_c, cb_c, wout_c,
                             state_conv[layer, :, 0], state_conv[layer, :, 1])
        xs = ys.reshape(1, DB, D)
        cv_p.append(stp.transpose(0, 2, 1, 3).reshape(B, CONV_W - 1, F2))
        cv_s.append(jnp.stack([state_conv[layer, :, 1], us], axis=1))

    y_prompt = _final_norm(xp, norm_final, n_meta, S_p, tm)
    y_sample = _final_norm(xs, norm_final, 0, DB, DB).reshape(DB, S_s, D)
    return (y_prompt, y_sample,
            jnp.stack(dk_p), jnp.stack(dv_p), jnp.stack(di_p),
            jnp.stack(dk_s), jnp.stack(dv_s), jnp.stack(di_s),
            jnp.stack(sk_p), jnp.stack(sv_p), jnp.stack(sk_s), jnp.stack(sv_s),
            jnp.stack(cv_p), jnp.stack(cv_s))
```
